```python
import jax, jax.numpy as jnp
from jax import lax
import numpy as np

D_MODEL = 1024
BATCH = 4
SEQ = 8192
DEPTH = 1
DEC_BATCH = 32
DEC_SEQ = 2048
PAST_LEN = 128

D_MIX = D_MODEL
HEAD_DIM = 64
D_GMLP = D_MIX // 2
D_FOURIER = D_MIX - D_GMLP
N_GMLP_HEADS = D_GMLP // HEAD_DIM
N_FOURIER_HEADS = D_FOURIER // HEAD_DIM
D_IN_PROJ = 2 * D_GMLP + D_FOURIER
CHUNK = 128
N_GROUPS = 4
EXPERTS_PER_GROUP = 8
N_EXPERTS = N_GROUPS * EXPERTS_PER_GROUP
TOP_K_IN_GROUP = 2
D_EXPERT = D_MODEL // 4
EPS = 1e-6

kernel_name = "hymba_gmlp_fnet_hmoe_encoder"


def rmsnorm(x, g):
    xf = x.astype(jnp.float32)
    y = xf * lax.rsqrt(jnp.mean(xf * xf, axis=-1, keepdims=True) + EPS)
    return (y * g.astype(jnp.float32)).astype(x.dtype)


def token_mixer(h, w_in, v_norm_g, spatial_w, spatial_b, fourier_w,
                gmlp_out_g, fourier_out_g, w_out):
    B, S, _ = h.shape
    z = jnp.einsum('bsd,de->bse', h, w_in)
    u = jax.nn.gelu(z[..., :D_GMLP])
    v = jax.nn.gelu(z[..., D_GMLP:2 * D_GMLP])
    v = rmsnorm(v.reshape(B, S, N_GMLP_HEADS, HEAD_DIM),
                v_norm_g.reshape(N_GMLP_HEADS, HEAD_DIM))
    v = v.reshape(B, S // CHUNK, CHUNK, N_GMLP_HEADS, HEAD_DIM)
    sv = (jnp.einsum('hpq,bnqhd->bnphd', spatial_w, v)
          + spatial_b.T[None, None, :, :, None].astype(v.dtype))
    y_g = u * sv.reshape(B, S, D_GMLP)
    f = z[..., 2 * D_GMLP:].reshape(B, S, N_FOURIER_HEADS, HEAD_DIM).astype(jnp.float32)
    fr = jnp.fft.fft2(f, axes=(1, 3), norm='ortho').real.astype(h.dtype)
    y_f = jnp.einsum('bshd,hde->bshe', fr, fourier_w).reshape(B, S, D_FOURIER)
    y = jnp.concatenate([rmsnorm(y_g, gmlp_out_g), rmsnorm(y_f, fourier_out_g)], axis=-1)
    return jnp.einsum('bse,ed->bsd', y, w_out)


def hier_moe(h, router_group_w, router_group_b, router_expert_w, router_expert_b,
             expert_w_gate, expert_w_up, expert_w_down):
    B, S, D = h.shape
    t = h.reshape(B * S, D)
    g_logits = (jnp.einsum('td,dg->tg', t, router_group_w)
                + router_group_b).astype(jnp.float32)
    g_probs = jax.nn.softmax(g_logits, axis=-1)
    g_idx = jnp.argmax(g_logits, axis=-1)
    g_p = jnp.take_along_axis(g_probs, g_idx[:, None], axis=-1)
    e_logits = (jnp.einsum('td,gde->tge', t, router_expert_w)
                + router_expert_b).astype(jnp.float32)
    e_logits = jnp.take_along_axis(e_logits, g_idx[:, None, None], axis=1)[:, 0]
    top_v, top_i = lax.top_k(e_logits, TOP_K_IN_GROUP)
    w = jax.nn.softmax(top_v, axis=-1) * g_p
    eid = g_idx[:, None] * EXPERTS_PER_GROUP + top_i
    gates = jnp.sum(jax.nn.one_hot(eid, N_EXPERTS, dtype=jnp.float32) * w[..., None],
                    axis=1).astype(h.dtype)
    out = jnp.zeros_like(t)
    for e in range(N_EXPERTS):
        a = jax.nn.silu(t @ expert_w_gate[e]) * (t @ expert_w_up[e])
        out = out + gates[:, e:e + 1] * (a @ expert_w_down[e])
    return out.reshape(B, S, D)


def trunk(x, norm_mix_g, w_in, v_norm_g, spatial_w, spatial_b, fourier_w,
          gmlp_out_g, fourier_out_g, w_out, norm_moe_g, router_group_w,
          router_group_b, router_expert_w, router_expert_b, expert_w_gate,
          expert_w_up, expert_w_down, final_norm_g):
    h = x
    for l in range(DEPTH):
        h = h + token_mixer(rmsnorm(h, norm_mix_g[l]), w_in[l], v_norm_g[l], spatial_w[l],
                            spatial_b[l], fourier_w[l], gmlp_out_g[l], fourier_out_g[l],
                            w_out[l])
        h = h + hier_moe(rmsnorm(h, norm_moe_g[l]), router_group_w[l], router_group_b[l],
                         router_expert_w[l], router_expert_b[l], expert_w_gate[l],
                         expert_w_up[l], expert_w_down[l])
    return rmsnorm(h, final_norm_g)


def setup_inputs(seed: int = 0) -> dict:
    key = jax.random.key(seed)
    ks = jax.random.split(key, 20)
    nrm = jax.random.normal
    L = DEPTH
    f32 = jnp.float32
    return {
        "x_prompt": nrm(ks[0], (BATCH, SEQ, D_MODEL), f32),
        "x_sample": nrm(ks[1], (DEC_BATCH, DEC_SEQ, D_MODEL), f32),
        "norm_mix_g": 1.0 + 0.01 * nrm(ks[2], (L, D_MODEL), f32),
        "w_in": nrm(ks[3], (L, D_MODEL, D_IN_PROJ), f32) * D_MODEL ** -0.5,
        "v_norm_g": 1.0 + 0.01 * nrm(ks[4], (L, D_GMLP), f32),
        "spatial_w": nrm(ks[5], (L, N_GMLP_HEADS, CHUNK, CHUNK), f32) * (0.5 * CHUNK ** -0.5),
        "spatial_b": 1.0 + 0.01 * nrm(ks[6], (L, N_GMLP_HEADS, CHUNK), f32),
        "fourier_w": nrm(ks[7], (L, N_FOURIER_HEADS, HEAD_DIM, HEAD_DIM), f32) * HEAD_DIM ** -0.5,
        "gmlp_out_g": 1.0 + 0.01 * nrm(ks[8], (L, D_GMLP), f32),
        "fourier_out_g": 1.0 + 0.01 * nrm(ks[9], (L, D_FOURIER), f32),
        "w_out": nrm(ks[10], (L, D_MIX, D_MODEL), f32) * D_MIX ** -0.5,
        "norm_moe_g": 1.0 + 0.01 * nrm(ks[11], (L, D_MODEL), f32),
        "router_group_w": nrm(ks[12], (L, D_MODEL, N_GROUPS), f32) * D_MODEL ** -0.5,
        "router_group_b": 0.01 * nrm(ks[13], (L, N_GROUPS), f32),
        "router_expert_w": nrm(ks[14], (L, N_GROUPS, D_MODEL, EXPERTS_PER_GROUP), f32) * D_MODEL ** -0.5,
        "router_expert_b": 0.01 * nrm(ks[15], (L, N_GROUPS, EXPERTS_PER_GROUP), f32),
        "expert_w_gate": nrm(ks[16], (L, N_EXPERTS, D_MODEL, D_EXPERT), f32) * D_MODEL ** -0.5,
        "expert_w_up": nrm(ks[17], (L, N_EXPERTS, D_MODEL, D_EXPERT), f32) * D_MODEL ** -0.5,
        "expert_w_down": nrm(ks[18], (L, N_EXPERTS, D_EXPERT, D_MODEL), f32) * D_EXPERT ** -0.5,
        "final_norm_g": 1.0 + 0.01 * nrm(ks[19], (D_MODEL,), f32),
    }


def reference(x_prompt, x_sample, norm_mix_g, w_in, v_norm_g, spatial_w, spatial_b,
              fourier_w, gmlp_out_g, fourier_out_g, w_out, norm_moe_g, router_group_w,
              router_group_b, router_expert_w, router_expert_b, expert_w_gate,
              expert_w_up, expert_w_down, final_norm_g):
    y_prompt = trunk(x_prompt, norm_mix_g, w_in, v_norm_g, spatial_w, spatial_b, fourier_w,
                     gmlp_out_g, fourier_out_g, w_out, norm_moe_g, router_group_w,
                     router_group_b, router_expert_w, router_expert_b, expert_w_gate,
                     expert_w_up, expert_w_down, final_norm_g)
    y_sample = trunk(x_sample, norm_mix_g, w_in, v_norm_g, spatial_w, spatial_b, fourier_w,
                     gmlp_out_g, fourier_out_g, w_out, norm_moe_g, router_group_w,
                     router_group_b, router_expert_w, router_expert_b, expert_w_gate,
                     expert_w_up, expert_w_down, final_norm_g)
    return (y_prompt, y_sample)
```

```python
import functools
import math

import numpy as np
import jax
import jax.numpy as jnp
from jax import lax
from jax.experimental import pallas as pl
from jax.experimental.pallas import tpu as pltpu

D_MODEL = 1024
HEAD_DIM = 64
N_HEADS = 8
D_GMLP = N_HEADS * HEAD_DIM
D_FOURIER = N_HEADS * HEAD_DIM
CHUNK = 128
N_GROUPS = 4
EXPERTS_PER_GROUP = 8
N_EXPERTS = N_GROUPS * EXPERTS_PER_GROUP
D_EXPERT = 256
EPS = 1e-6

LANES = 128
FFT_N1 = 128
MIX_IN_ROWS = 512
MIX_OUT_ROWS = 512
FFT1_LANES = 4096
MOE_ROWS = 1024
ROUTER_LANES = LANES
ROUTER_EXPERT_LANE0 = N_GROUPS
VMEM_LIMIT = 48 * 1024 * 1024

F32 = jnp.float32
BF16 = jnp.bfloat16


def _rms(v):
    return v * lax.rsqrt(jnp.mean(v * v, axis=-1, keepdims=True) + EPS)


def _gelu_tanh(v):
    c = math.sqrt(2.0 / math.pi)
    return v * (0.5 * (1.0 + jnp.tanh(c * (v + 0.044715 * (v * v * v)))))


def _dot(a, b):
    return jnp.dot(a, b, preferred_element_type=F32)


@functools.lru_cache(maxsize=None)
def _dft64_blockdiag():
    d = np.arange(HEAD_DIM)
    ang = 2.0 * np.pi * np.outer(d, d) / HEAD_DIM
    eye = np.eye(N_HEADS)
    return (np.kron(eye, np.cos(ang)).astype(np.float32),
            np.kron(eye, np.sin(ang)).astype(np.float32))


@functools.lru_cache(maxsize=None)
def _head_mean_matrix():
    return np.kron(np.eye(N_HEADS), np.full((HEAD_DIM, HEAD_DIM), 1.0 / HEAD_DIM)).astype(np.float32)


@functools.lru_cache(maxsize=None)
def _fft_stage1_matrix():
    k = np.arange(FFT_N1)
    ang = 2.0 * np.pi * np.outer(k, k) / FFT_N1
    c, s = np.cos(ang), np.sin(ang)
    return np.block([[c, -s], [-s, -c]]).astype(np.float32)


@functools.lru_cache(maxsize=None)
def _fft_stage2_matrices(seq):
    n2 = seq // FFT_N1
    k1 = np.arange(FFT_N1)[:, None, None]
    k2 = np.arange(n2)[None, :, None]
    m = np.arange(n2)[None, None, :]
    ang = 2.0 * np.pi * ((m * (k1 + FFT_N1 * k2)) % seq) / seq
    return np.concatenate([np.cos(ang), np.sin(ang)], axis=-1).astype(np.float32)


def _fold_kernel(wf_ref, fw_ref, cbd_ref, sbd_ref, wp_ref, wq_ref):
    hp = lax.Precision.HIGHEST
    fw = fw_ref[...]
    mc = jnp.dot(cbd_ref[...], fw, precision=hp, preferred_element_type=F32)
    ms = jnp.dot(sbd_ref[...], fw, precision=hp, preferred_element_type=F32)
    wf = wf_ref[...]
    wp_ref[...] = jnp.dot(wf, mc, precision=hp, preferred_element_type=F32).astype(BF16)
    wq_ref[...] = jnp.dot(wf, ms, precision=hp, preferred_element_type=F32).astype(BF16)


def _fold_fourier_weights(w_in_f, fourier_w):
    cbd, sbd = _dft64_blockdiag()
    fw_bd = jax.scipy.linalg.block_diag(*[fourier_w[h] for h in range(N_HEADS)])
    out = jax.ShapeDtypeStruct((D_MODEL, D_FOURIER), BF16)
    return pl.pallas_call(
        _fold_kernel,
        out_shape=(out, out),
        compiler_params=pltpu.CompilerParams(vmem_limit_bytes=VMEM_LIMIT),
        name="fold_fourier",
    )(w_in_f, fw_bd, jnp.asarray(cbd), jnp.asarray(sbd))


def _mix_in_kernel(x_ref, g_ref, w_ref, hm_ref, vg_ref, wsp_ref, sb_ref, og_ref,
                   yg_ref, p_ref, q_ref):
    rows = x_ref.shape[0]
    xn = _rms(x_ref[...]) * g_ref[...]
    z = _dot(xn.astype(BF16), w_ref[...])
    u = _gelu_tanh(z[:, :D_GMLP])
    v = _gelu_tanh(z[:, D_GMLP:2 * D_GMLP])
    p_ref[...] = z[:, 2 * D_GMLP:2 * D_GMLP + D_FOURIER].astype(BF16)
    q_ref[...] = z[:, 2 * D_GMLP + D_FOURIER:].astype(BF16)
    head_ms = _dot((v * v).astype(BF16), hm_ref[...])
    vn = (v * lax.rsqrt(head_ms + EPS) * vg_ref[...]).astype(BF16)
    low_head = lax.broadcasted_iota(jnp.int32, (CHUNK, LANES), 1) < HEAD_DIM
    zero = jnp.zeros((CHUNK, LANES), BF16)
    for c in range(rows // CHUNK):
        r = slice(c * CHUNK, (c + 1) * CHUNK)
        parts = []
        for j in range(D_GMLP // LANES):
            vp = vn[r, j * LANES:(j + 1) * LANES]
            rhs = jnp.concatenate([jnp.where(low_head, vp, zero),
                                   jnp.where(low_head, zero, vp)], axis=0)
            parts.append(_dot(wsp_ref[j], rhs))
        sv = jnp.concatenate(parts, axis=1) + sb_ref[...]
        yg = u[r] * sv
        yg_ref[r, :] = (_rms(yg) * og_ref[...]).astype(BF16)


def _mix_in(x2d, norm_g, w_ext, v_norm_g, wsp, sbias, gmlp_out_g):
    t = x2d.shape[0]
    rows = MIX_IN_ROWS
    const = lambda shape: pl.BlockSpec(shape, lambda i: (0,) * len(shape))
    tok = lambda width: pl.BlockSpec((rows, width), lambda i: (i, 0))
    return pl.pallas_call(
        _mix_in_kernel,
        grid=(t // rows,),
        in_specs=[tok(D_MODEL), const((1, D_MODEL)), const(w_ext.shape),
                  const((D_GMLP, D_GMLP)), const((1, D_GMLP)), const(wsp.shape),
                  const((CHUNK, D_GMLP)), const((1, D_GMLP))],
        out_specs=[tok(D_GMLP), tok(D_FOURIER), tok(D_FOURIER)],
        out_shape=[jax.ShapeDtypeStruct((t, D_GMLP), BF16),
                   jax.ShapeDtypeStruct((t, D_FOURIER), BF16),
                   jax.ShapeDtypeStruct((t, D_FOURIER), BF16)],
        compiler_params=pltpu.CompilerParams(dimension_semantics=("arbitrary",),
                                             vmem_limit_bytes=VMEM_LIMIT),
        name="mix_in",
    )(x2d, norm_g, w_ext, jnp.asarray(_head_mean_matrix(), BF16), v_norm_g, wsp, sbias, gmlp_out_g)


def _fft1_kernel(g1_ref, p_ref, q_ref, ar_ref, ai_ref):
    rhs = jnp.concatenate([p_ref[0], q_ref[0]], axis=0)
    a = _dot(g1_ref[...], rhs)
    ar_ref[0] = a[:FFT_N1].astype(BF16)
    ai_ref[0] = a[FFT_N1:].astype(BF16)


def _fft1(p3, q3):
    b, _, width = p3.shape
    lanes = min(FFT1_LANES, width)
    blk = pl.BlockSpec((1, FFT_N1, lanes), lambda bi, j: (bi, 0, j))
    out = jax.ShapeDtypeStruct(p3.shape, BF16)
    return pl.pallas_call(
        _fft1_kernel,
        grid=(b, width // lanes),
        in_specs=[pl.BlockSpec((2 * FFT_N1, 2 * FFT_N1), lambda bi, j: (0, 0)), blk, blk],
        out_specs=[blk, blk],
        out_shape=[out, out],
        compiler_params=pltpu.CompilerParams(dimension_semantics=("arbitrary", "arbitrary"),
                                             vmem_limit_bytes=VMEM_LIMIT),
        name="fft1",
    )(jnp.asarray(_fft_stage1_matrix(), BF16), p3, q3)


def _route(logits):
    lane = lax.broadcasted_iota(jnp.int32, logits.shape, 1)
    neg = jnp.float32(-jnp.inf)
    is_group = lane < N_GROUPS
    gl = jnp.where(is_group, logits, neg)
    gmax = jnp.max(gl, axis=-1, keepdims=True)
    gidx = jnp.min(jnp.where(gl == gmax, lane, ROUTER_LANES), axis=-1, keepdims=True)
    gden = jnp.sum(jnp.where(is_group, jnp.exp(gl - gmax), 0.0), axis=-1, keepdims=True)
    gp = 1.0 / gden
    lo = ROUTER_EXPERT_LANE0 + EXPERTS_PER_GROUP * gidx
    el = jnp.where((lane >= lo) & (lane < lo + EXPERTS_PER_GROUP), logits, neg)
    v1 = jnp.max(el, axis=-1, keepdims=True)
    i1 = jnp.min(jnp.where(el == v1, lane, ROUTER_LANES), axis=-1, keepdims=True)
    el2 = jnp.where(lane == i1, neg, el)
    v2 = jnp.max(el2, axis=-1, keepdims=True)
    i2 = jnp.min(jnp.where(el2 == v2, lane, ROUTER_LANES), axis=-1, keepdims=True)
    e21 = jnp.exp(v2 - v1)
    den = 1.0 + e21
    w1 = gp / den
    w2 = gp * e21 / den
    return jnp.where(lane == i1, w1, 0.0) + jnp.where(lane == i2, w2, 0.0)


def _mix_out_kernel(scale, g3_ref, ar_ref, ai_ref, yg_ref, x_ref, fg_ref, wout_ref, mg_ref,
                    wr_ref, br_ref, h_ref, t_ref, gates_ref):
    tk1, n2 = g3_ref.shape[0], g3_ref.shape[1]
    ys = []
    for j in range(tk1):
        rhs = jnp.concatenate([ar_ref[0, j], ai_ref[0, j]], axis=0)
        ys.append(_dot(g3_ref[j], rhs))
    yf = jnp.concatenate(ys, axis=0) * scale
    yfn = (_rms(yf) * fg_ref[...]).astype(BF16)
    yg = jnp.concatenate([yg_ref[0, :, j * D_GMLP:(j + 1) * D_GMLP] for j in range(tk1)], axis=0)
    x = jnp.concatenate([x_ref[0, :, j * D_MODEL:(j + 1) * D_MODEL] for j in range(tk1)], axis=0)
    h = x + _dot(jnp.concatenate([yg, yfn], axis=1), wout_ref[...])
    t = _rms(h) * mg_ref[...]
    logits = jnp.dot(t, wr_ref[...], precision=lax.Precision.HIGHEST,
                     preferred_element_type=F32) + br_ref[...]
    gates = _route(logits)
    tb = t.astype(BF16)
    for j in range(tk1):
        r = slice(j * n2, (j + 1) * n2)
        h_ref[0, :, j * D_MODEL:(j + 1) * D_MODEL] = h[r]
        t_ref[0, :, j * D_MODEL:(j + 1) * D_MODEL] = tb[r]
        gates_ref[0, :, j * ROUTER_LANES:(j + 1) * ROUTER_LANES] = gates[r]


def _mix_out(ar, ai, yg, x, fourier_out_g, w_out, norm_moe_g, wr, br):
    b, seq, _ = x.shape
    n2 = seq // FFT_N1
    tk1 = MIX_OUT_ROWS // n2
    g3 = jnp.asarray(_fft_stage2_matrices(seq), BF16)
    scale = 1.0 / math.sqrt(HEAD_DIM * seq)
    strided = lambda width: pl.BlockSpec((1, n2, tk1 * width), lambda bi, i: (bi, 0, i))
    stage1 = pl.BlockSpec((1, tk1, n2, D_FOURIER), lambda bi, i: (bi, i, 0, 0))
    const = lambda shape: pl.BlockSpec(shape, lambda bi, i: (0,) * len(shape))
    h, t, gates = pl.pallas_call(
        functools.partial(_mix_out_kernel, scale),
        grid=(b, FFT_N1 // tk1),
        in_specs=[pl.BlockSpec((tk1, n2, 2 * n2), lambda bi, i: (i, 0, 0)), stage1, stage1,
                  strided(D_GMLP), strided(D_MODEL), const((1, D_FOURIER)),
                  const((D_MODEL, D_MODEL)), const((1, D_MODEL)),
                  const((D_MODEL, ROUTER_LANES)), const((1, ROUTER_LANES))],
        out_specs=[strided(D_MODEL), strided(D_MODEL), strided(ROUTER_LANES)],
        out_shape=[jax.ShapeDtypeStruct((b, n2, FFT_N1 * D_MODEL), F32),
                   jax.ShapeDtypeStruct((b, n2, FFT_N1 * D_MODEL), BF16),
                   jax.ShapeDtypeStruct((b, n2, FFT_N1 * ROUTER_LANES), F32)],
        compiler_params=pltpu.CompilerParams(dimension_semantics=("arbitrary", "arbitrary"),
                                             vmem_limit_bytes=VMEM_LIMIT),
        name="mix_out",
    )(g3, ar.reshape(b, FFT_N1, n2, D_FOURIER), ai.reshape(b, FFT_N1, n2, D_FOURIER),
      yg.reshape(b, n2, FFT_N1 * D_GMLP), x.reshape(b, n2, FFT_N1 * D_MODEL),
      fourier_out_g, w_out, norm_moe_g, wr, br)
    tokens = b * seq
    return (h.reshape(tokens, D_MODEL), t.reshape(tokens, D_MODEL),
            gates.reshape(tokens, ROUTER_LANES))


def _moe_kernel(t_ref, gates_ref, h_ref, wgu_ref, wd_ref, fg_ref, out_ref, acc_ref):
    e = pl.program_id(1)

    @pl.when(e == 0)
    def _():
        acc_ref[...] = jnp.zeros_like(acc_ref)

    gu = _dot(t_ref[...], wgu_ref[0])
    gate, up = gu[:, :D_EXPERT], gu[:, D_EXPERT:]
    act = gate * (1.0 / (1.0 + jnp.exp(-gate))) * up
    lane = lax.broadcasted_iota(jnp.int32, gates_ref.shape, 1)
    col = jnp.sum(jnp.where(lane == e + ROUTER_EXPERT_LANE0, gates_ref[...], 0.0),
                  axis=-1, keepdims=True)
    acc_ref[...] += _dot((act * col).astype(BF16), wd_ref[0])

    @pl.when(e == N_EXPERTS - 1)
    def _():
        out_ref[...] = _rms(h_ref[...] + acc_ref[...]) * fg_ref[...]


def _moe_dense(t, gates, h, wgu, wd, final_g):
    tokens = t.shape[0]
    rows = MOE_ROWS
    tok = lambda width: pl.BlockSpec((rows, width), lambda i, e: (i, 0))
    return pl.pallas_call(
        _moe_kernel,
        grid=(tokens // rows, N_EXPERTS),
        in_specs=[tok(D_MODEL), tok(ROUTER_LANES), tok(D_MODEL),
                  pl.BlockSpec((1, D_MODEL, 2 * D_EXPERT), lambda i, e: (e, 0, 0)),
                  pl.BlockSpec((1, D_EXPERT, D_MODEL), lambda i, e: (e, 0, 0)),
                  pl.BlockSpec((1, D_MODEL), lambda i, e: (0, 0))],
        out_specs=tok(D_MODEL),
        out_shape=jax.ShapeDtypeStruct((tokens, D_MODEL), F32),
        scratch_shapes=[pltpu.VMEM((rows, D_MODEL), F32)],
        compiler_params=pltpu.CompilerParams(dimension_semantics=("arbitrary", "arbitrary"),
                                             vmem_limit_bytes=VMEM_LIMIT),
        name="moe_dense",
    )(t, gates, h, wgu, wd, final_g)


def _prepare_weights(norm_mix_g, w_in, v_norm_g, spatial_w, spatial_b, fourier_w, gmlp_out_g,
                     fourier_out_g, w_out, norm_moe_g, router_group_w, router_group_b,
                     router_expert_w, router_expert_b, expert_w_gate, expert_w_up,
                     expert_w_down, final_norm_g):
    w_in, spatial_w, spatial_b, fourier_w = w_in[0], spatial_w[0], spatial_b[0], fourier_w[0]
    wp, wq = _fold_fourier_weights(w_in[:, 2 * D_GMLP:], fourier_w)
    w_ext = jnp.concatenate([w_in[:, :2 * D_GMLP].astype(BF16), wp, wq], axis=1)
    wsp = spatial_w.reshape(N_HEADS // 2, 2, CHUNK, CHUNK).transpose(0, 2, 1, 3)
    wsp = wsp.reshape(N_HEADS // 2, CHUNK, 2 * CHUNK).astype(BF16)
    sbias = jnp.repeat(spatial_b.T, HEAD_DIM, axis=1)
    pad = ROUTER_LANES - N_GROUPS - N_EXPERTS
    wr = jnp.concatenate([router_group_w[0],
                          router_expert_w[0].transpose(1, 0, 2).reshape(D_MODEL, N_EXPERTS),
                          jnp.zeros((D_MODEL, pad), F32)], axis=1)
    br = jnp.concatenate([router_group_b[0], router_expert_b[0].reshape(N_EXPERTS),
                          jnp.zeros((pad,), F32)])[None, :]
    wgu = jnp.concatenate([expert_w_gate[0], expert_w_up[0]], axis=-1).astype(BF16)
    wd = expert_w_down[0].astype(BF16)
    row = lambda a: a.reshape(1, -1)
    return dict(norm_mix_g=row(norm_mix_g), w_ext=w_ext, v_norm_g=row(v_norm_g), wsp=wsp,
                sbias=sbias, gmlp_out_g=row(gmlp_out_g), fourier_out_g=row(fourier_out_g),
                w_out=w_out[0].astype(BF16), norm_moe_g=row(norm_moe_g), wr=wr, br=br,
                wgu=wgu, wd=wd, final_norm_g=row(final_norm_g))


def _trunk(x, w):
    b, seq, _ = x.shape
    tokens = b * seq
    n2 = seq // FFT_N1
    yg, p, q = _mix_in(x.reshape(tokens, D_MODEL), w["norm_mix_g"], w["w_ext"], w["v_norm_g"],
                       w["wsp"], w["sbias"], w["gmlp_out_g"])
    ar, ai = _fft1(p.reshape(b, FFT_N1, n2 * D_FOURIER), q.reshape(b, FFT_N1, n2 * D_FOURIER))
    h, t, gates = _mix_out(ar.reshape(b, seq, D_FOURIER), ai.reshape(b, seq, D_FOURIER),
                           yg.reshape(b, seq, D_GMLP), x, w["fourier_out_g"], w["w_out"],
                           w["norm_moe_g"], w["wr"], w["br"])
    y = _moe_dense(t, gates, h, w["wgu"], w["wd"], w["final_norm_g"])
    return y.reshape(b, seq, D_MODEL)


def kernel(x_prompt, x_sample, norm_mix_g, w_in, v_norm_g, spatial_w, spatial_b, fourier_w,
           gmlp_out_g, fourier_out_g, w_out, norm_moe_g, router_group_w, router_group_b,
           router_expert_w, router_expert_b, expert_w_gate, expert_w_up, expert_w_down,
           final_norm_g):
    w = _prepare_weights(norm_mix_g, w_in, v_norm_g, spatial_w, spatial_b, fourier_w, gmlp_out_g,
                         fourier_out_g, w_out, norm_moe_g, router_group_w, router_group_b,
                         router_expert_w, router_expert_b, expert_w_gate, expert_w_up,
                         expert_w_down, final_norm_g)
    return (_trunk(x_prompt, w), _trunk(x_sample, w))
```

```python
import functools
import math

import numpy as np
import jax
import jax.numpy as jnp
from jax import lax
from jax.experimental import pallas as pl
from jax.experimental.pallas import tpu as pltpu

D_MODEL = 1024
HEAD_DIM = 64
N_HEADS = 8
D_GMLP = N_HEADS * HEAD_DIM
D_FOURIER = N_HEADS * HEAD_DIM
CHUNK = 128
N_GROUPS = 4
EXPERTS_PER_GROUP = 8
N_EXPERTS = N_GROUPS * EXPERTS_PER_GROUP
TOP_K = 2
D_EXPERT = 256
EPS = 1e-6

LANES = 128
BF16_SUBLANES = 16
FFT_N2 = 32
FFT_K1_BLOCK = BF16_SUBLANES
MIX_IN_ROWS = 512
FFT1_LANES = 4096
MOE_TILE = 256
COMBINE_ROWS = 256
ROUTE_LANES = LANES
ROUTE_EXPERT_LANE0 = N_GROUPS
DMA_ISSUE_UNROLL = 8
VMEM_LIMIT = 48 * 1024 * 1024

F32 = jnp.float32
BF16 = jnp.bfloat16


def _rms(v):
    return v * lax.rsqrt(jnp.mean(v * v, axis=-1, keepdims=True) + EPS)


def _gelu_tanh(v):
    c = math.sqrt(2.0 / math.pi)
    return v * (0.5 * (1.0 + jnp.tanh(c * (v + 0.044715 * (v * v * v)))))


def _dot(a, b):
    return jnp.dot(a, b, preferred_element_type=F32)


@functools.lru_cache(maxsize=None)
def _dft64_blockdiag():
    d = np.arange(HEAD_DIM)
    ang = 2.0 * np.pi * np.outer(d, d) / HEAD_DIM
    eye = np.eye(N_HEADS)
    return (np.kron(eye, np.cos(ang)).astype(np.float32),
            np.kron(eye, np.sin(ang)).astype(np.float32))


@functools.lru_cache(maxsize=None)
def _head_mean_matrix():
    return np.kron(np.eye(N_HEADS), np.full((HEAD_DIM, HEAD_DIM), 1.0 / HEAD_DIM)).astype(np.float32)


@functools.lru_cache(maxsize=None)
def _fft_stage1_matrix(n1):
    k = np.arange(n1)
    ang = 2.0 * np.pi * ((np.outer(k, k)) % n1) / n1
    c, s = np.cos(ang), np.sin(ang)
    return np.block([[c, -s], [-s, -c]]).astype(np.float32)


@functools.lru_cache(maxsize=None)
def _fft_stage2_matrices(seq):
    n1 = seq // FFT_N2
    k1 = np.arange(n1)[:, None, None]
    k2 = np.arange(FFT_N2)[None, :, None]
    m = np.arange(FFT_N2)[None, None, :]
    ang = 2.0 * np.pi * ((m * (k1 + n1 * k2)) % seq) / seq
    return np.stack([np.cos(ang), np.sin(ang)], axis=2).astype(np.float32)


def _fft_stage2_blocks(seq):
    n1 = seq // FFT_N2
    g = jnp.asarray(_fft_stage2_matrices(seq)).reshape(n1 // FFT_K1_BLOCK, FFT_K1_BLOCK,
                                                       FFT_N2, 2, FFT_N2)
    eye = jnp.eye(FFT_K1_BLOCK, dtype=F32)
    big = g[:, :, :, :, None, :] * eye[None, :, None, None, :, None]
    big = big.transpose(0, 2, 1, 3, 4, 5)
    return big.reshape(n1 // FFT_K1_BLOCK, FFT_N2 * FFT_K1_BLOCK,
                       2 * FFT_K1_BLOCK * FFT_N2).astype(BF16)


def _fold_kernel(wf_ref, fw_ref, cbd_ref, sbd_ref, wp_ref, wq_ref):
    hp = lax.Precision.HIGHEST
    fw = fw_ref[...]
    mc = jnp.dot(cbd_ref[...], fw, precision=hp, preferred_element_type=F32)
    ms = jnp.dot(sbd_ref[...], fw, precision=hp, preferred_element_type=F32)
    wf = wf_ref[...]
    wp_ref[...] = jnp.dot(wf, mc, precision=hp, preferred_element_type=F32).astype(BF16)
    wq_ref[...] = jnp.dot(wf, ms, precision=hp, preferred_element_type=F32).astype(BF16)


def _fold_fourier_weights(w_in_f, fourier_w):
    cbd, sbd = _dft64_blockdiag()
    fw_bd = jax.scipy.linalg.block_diag(*[fourier_w[h] for h in range(N_HEADS)])
    out = jax.ShapeDtypeStruct((D_MODEL, D_FOURIER), BF16)
    return pl.pallas_call(
        _fold_kernel,
        out_shape=(out, out),
        compiler_params=pltpu.CompilerParams(vmem_limit_bytes=VMEM_LIMIT),
        name="fold_fourier",
    )(w_in_f, fw_bd, jnp.asarray(cbd), jnp.asarray(sbd))


def _mix_in_kernel(x_ref, g_ref, w_ref, hm_ref, vg_ref, wsp_ref, sb_ref, og_ref,
                   yg_ref, p_ref, q_ref):
    rows = x_ref.shape[0]
    xn = _rms(x_ref[...]) * g_ref[...]
    z = _dot(xn.astype(BF16), w_ref[...])
    u = _gelu_tanh(z[:, :D_GMLP])
    v = _gelu_tanh(z[:, D_GMLP:2 * D_GMLP])
    p_ref[...] = z[:, 2 * D_GMLP:2 * D_GMLP + D_FOURIER].astype(BF16)
    q_ref[...] = z[:, 2 * D_GMLP + D_FOURIER:].astype(BF16)
    head_ms = _dot((v * v).astype(BF16), hm_ref[...])
    vn = (v * lax.rsqrt(head_ms + EPS) * vg_ref[...]).astype(BF16)
    low_head = lax.broadcasted_iota(jnp.int32, (CHUNK, LANES), 1) < HEAD_DIM
    zero = jnp.zeros((CHUNK, LANES), BF16)
    for c in range(rows // CHUNK):
        r = slice(c * CHUNK, (c + 1) * CHUNK)
        parts = []
        for j in range(D_GMLP // LANES):
            vp = vn[r, j * LANES:(j + 1) * LANES]
            rhs = jnp.concatenate([jnp.where(low_head, vp, zero),
                                   jnp.where(low_head, zero, vp)], axis=0)
            parts.append(_dot(wsp_ref[j], rhs))
        sv = jnp.concatenate(parts, axis=1) + sb_ref[...]
        yg = u[r] * sv
        yg_ref[r, :] = (_rms(yg) * og_ref[...]).astype(BF16)


def _mix_in(x2d, norm_g, w_ext, v_norm_g, wsp, sbias, gmlp_out_g):
    t = x2d.shape[0]
    rows = MIX_IN_ROWS
    const = lambda shape: pl.BlockSpec(shape, lambda i: (0,) * len(shape))
    tok = lambda width: pl.BlockSpec((rows, width), lambda i: (i, 0))
    return pl.pallas_call(
        _mix_in_kernel,
        grid=(t // rows,),
        in_specs=[tok(D_MODEL), const((1, D_MODEL)), const(w_ext.shape),
                  const((D_GMLP, D_GMLP)), const((1, D_GMLP)), const(wsp.shape),
                  const((CHUNK, D_GMLP)), const((1, D_GMLP))],
        out_specs=[tok(D_GMLP), tok(D_FOURIER), tok(D_FOURIER)],
        out_shape=[jax.ShapeDtypeStruct((t, D_GMLP), BF16),
                   jax.ShapeDtypeStruct((t, D_FOURIER), BF16),
                   jax.ShapeDtypeStruct((t, D_FOURIER), BF16)],
        compiler_params=pltpu.CompilerParams(dimension_semantics=("arbitrary",),
                                             vmem_limit_bytes=VMEM_LIMIT),
        name="mix_in",
    )(x2d, norm_g, w_ext, jnp.asarray(_head_mean_matrix(), BF16), v_norm_g, wsp, sbias, gmlp_out_g)


def _fft1_kernel(g1_ref, p_ref, q_ref, ar_ref, ai_ref):
    n1 = p_ref.shape[1]
    rhs = jnp.concatenate([p_ref[0], q_ref[0]], axis=0)
    a = _dot(g1_ref[...], rhs)
    ar_ref[0] = a[:n1].astype(BF16)
    ai_ref[0] = a[n1:].astype(BF16)


def _fft1(p3, q3):
    b, n1, width = p3.shape
    lanes = min(FFT1_LANES, width)
    blk = pl.BlockSpec((1, n1, lanes), lambda bi, j: (bi, 0, j))
    out = jax.ShapeDtypeStruct(p3.shape, BF16)
    return pl.pallas_call(
        _fft1_kernel,
        grid=(b, width // lanes),
        in_specs=[pl.BlockSpec((2 * n1, 2 * n1), lambda bi, j: (0, 0)), blk, blk],
        out_specs=[blk, blk],
        out_shape=[out, out],
        compiler_params=pltpu.CompilerParams(dimension_semantics=("arbitrary", "arbitrary"),
                                             vmem_limit_bytes=VMEM_LIMIT),
        name="fft1",
    )(jnp.asarray(_fft_stage1_matrix(n1), BF16), p3, q3)


def _route(logits):
    lane = lax.broadcasted_iota(jnp.int32, logits.shape, 1)
    neg = jnp.float32(-jnp.inf)
    is_group = lane < N_GROUPS
    gl = jnp.where(is_group, logits, neg)
    gmax = jnp.max(gl, axis=-1, keepdims=True)
    gidx = jnp.min(jnp.where(gl == gmax, lane, ROUTE_LANES), axis=-1, keepdims=True)
    gden = jnp.sum(jnp.where(is_group, jnp.exp(gl - gmax), 0.0), axis=-1, keepdims=True)
    gp = 1.0 / gden
    lo = ROUTE_EXPERT_LANE0 + EXPERTS_PER_GROUP * gidx
    el = jnp.where((lane >= lo) & (lane < lo + EXPERTS_PER_GROUP), logits, neg)
    v1 = jnp.max(el, axis=-1, keepdims=True)
    i1 = jnp.min(jnp.where(el == v1, lane, ROUTE_LANES), axis=-1, keepdims=True)
    el2 = jnp.where(lane == i1, neg, el)
    v2 = jnp.max(el2, axis=-1, keepdims=True)
    i2 = jnp.min(jnp.where(el2 == v2, lane, ROUTE_LANES), axis=-1, keepdims=True)
    e21 = jnp.exp(v2 - v1)
    den = 1.0 + e21
    w1 = gp / den
    w2 = gp * e21 / den
    e1 = (i1 - ROUTE_EXPERT_LANE0).astype(F32)
    e2 = (i2 - ROUTE_EXPERT_LANE0).astype(F32)
    return jnp.where(lane == 0, e1, jnp.where(lane == 1, e2, jnp.where(lane == 2, w1,
                     jnp.where(lane == 3, w2, 0.0))))


def _mix_out_kernel(scale, gb_ref, ar_ref, ai_ref, yg_ref, x_ref, fg_ref, wout_ref, mg_ref,
                    wr_ref, br_ref, h_ref, route_ref):
    n2, tk1 = x_ref.shape[1], x_ref.shape[2]
    rows = n2 * tk1
    stage1 = jnp.concatenate([ar_ref[0].reshape(tk1 * n2, D_FOURIER),
                              ai_ref[0].reshape(tk1 * n2, D_FOURIER)], axis=0)
    yf = _dot(gb_ref[0], stage1) * scale
    yfn = (_rms(yf) * fg_ref[...]).astype(BF16)
    yg = yg_ref[0].reshape(rows, D_GMLP)
    x = x_ref[0].reshape(rows, D_MODEL)
    h = x + _dot(jnp.concatenate([yg, yfn], axis=1), wout_ref[...])
    h_ref[0] = h.reshape(n2, tk1, D_MODEL)
    t = _rms(h) * mg_ref[...]
    logits = jnp.dot(t, wr_ref[...], precision=lax.Precision.HIGHEST,
                     preferred_element_type=F32) + br_ref[...]
    route_ref[0] = _route(logits).reshape(n2, tk1, ROUTE_LANES)


def _mix_out(ar, ai, yg, x, fourier_out_g, w_out, norm_moe_g, wr, br):
    b, seq, _ = x.shape
    n1 = seq // FFT_N2
    tk1 = FFT_K1_BLOCK
    scale = 1.0 / math.sqrt(HEAD_DIM * seq)
    seqrows = lambda width: pl.BlockSpec((1, FFT_N2, tk1, width), lambda bi, i: (bi, 0, i, 0))
    stage1 = pl.BlockSpec((1, tk1, FFT_N2, D_FOURIER), lambda bi, i: (bi, i, 0, 0))
    const = lambda shape: pl.BlockSpec(shape, lambda bi, i: (0,) * len(shape))
    rows, depth = FFT_N2 * tk1, 2 * tk1 * FFT_N2
    h, route = pl.pallas_call(
        functools.partial(_mix_out_kernel, scale),
        grid=(b, n1 // tk1),
        in_specs=[pl.BlockSpec((1, rows, depth), lambda bi, i: (i, 0, 0)), stage1, stage1,
                  seqrows(D_GMLP), seqrows(D_MODEL), const((1, D_FOURIER)),
                  const((D_MODEL, D_MODEL)), const((1, D_MODEL)),
                  const((D_MODEL, ROUTE_LANES)), const((1, ROUTE_LANES))],
        out_specs=[seqrows(D_MODEL), seqrows(ROUTE_LANES)],
        out_shape=[jax.ShapeDtypeStruct((b, FFT_N2, n1, D_MODEL), F32),
                   jax.ShapeDtypeStruct((b, FFT_N2, n1, ROUTE_LANES), F32)],
        compiler_params=pltpu.CompilerParams(dimension_semantics=("arbitrary", "arbitrary"),
                                             vmem_limit_bytes=VMEM_LIMIT),
        name="mix_out",
    )(_fft_stage2_blocks(seq), ar.reshape(b, n1, FFT_N2, D_FOURIER),
      ai.reshape(b, n1, FFT_N2, D_FOURIER), yg.reshape(b, FFT_N2, n1, D_GMLP),
      x.reshape(b, FFT_N2, n1, D_MODEL), fourier_out_g, w_out, norm_moe_g, wr, br)
    tokens = b * seq
    return h.reshape(tokens, D_MODEL), route.reshape(tokens, ROUTE_LANES)


def _route_plan(route):
    tokens = route.shape[0]
    pairs = TOP_K * tokens
    max_tiles = pairs // MOE_TILE + N_EXPERTS
    eid = route[:, :TOP_K].astype(jnp.int32).reshape(pairs)
    onehot = (eid[:, None] == jnp.arange(N_EXPERTS, dtype=jnp.int32)[None, :]).astype(jnp.int32)
    csum = jnp.cumsum(onehot, axis=0)
    rank = jnp.take_along_axis(csum, eid[:, None], axis=1)[:, 0] - 1
    counts = csum[-1]
    tiles_per = (counts + MOE_TILE - 1) // MOE_TILE
    tile_end = jnp.cumsum(tiles_per)
    tile_start = tile_end - tiles_per
    pos = tile_start[eid] * MOE_TILE + rank
    tile_expert = jnp.searchsorted(tile_end, jnp.arange(max_tiles, dtype=jnp.int32), side="right")
    tile_expert = jnp.minimum(tile_expert, N_EXPERTS - 1).astype(jnp.int32)
    src = jnp.zeros((max_tiles * MOE_TILE,), jnp.int32).at[pos].set(
        jnp.arange(pairs, dtype=jnp.int32) // TOP_K)
    return (pos.reshape(tokens, TOP_K), src.reshape(max_tiles, MOE_TILE), tile_expert,
            tile_end[-1:].astype(jnp.int32))


def _issue_row_gather(idx_ref, n_rows, src_hbm, dst_buf, slot, sem):
    def body(r, carry):
        row = idx_ref[0, 0, r]
        pltpu.make_async_copy(src_hbm.at[pl.ds(row, 1)], dst_buf.at[slot, pl.ds(r, 1)],
                              sem.at[slot]).start()
        return carry
    lax.fori_loop(0, n_rows, body, 0, unroll=DMA_ISSUE_UNROLL)


def _wait_row_gather(n_rows, src_hbm, dst_buf, slot, sem):
    pltpu.make_async_copy(src_hbm.at[pl.ds(0, n_rows)], dst_buf.at[slot], sem.at[slot]).wait()


def _moe_experts_kernel(te_ref, nv_ref, src_ref, src_next_ref, h_hbm, mg_ref, wgu_ref, wd_ref,
                        y_ref, xbuf, sem):
    del te_ref
    k = pl.program_id(0)
    n_valid = nv_ref[0]

    @pl.when(k == 0)
    def _():
        _issue_row_gather(src_ref, MOE_TILE, h_hbm, xbuf, 0, sem)

    @pl.when(k + 1 < n_valid)
    def _():
        _issue_row_gather(src_next_ref, MOE_TILE, h_hbm, xbuf, (k + 1) % 2, sem)

    @pl.when(k < n_valid)
    def _():
        slot = k % 2
        _wait_row_gather(MOE_TILE, h_hbm, xbuf, slot, sem)
        t = (_rms(xbuf[slot]) * mg_ref[...]).astype(BF16)
        gu = _dot(t, wgu_ref[0])
        gate, up = gu[:, :D_EXPERT], gu[:, D_EXPERT:]
        act = gate * (1.0 / (1.0 + jnp.exp(-gate))) * up
        y_ref[...] = _dot(act.astype(BF16), wd_ref[0])

    @pl.when(k >= n_valid)
    def _():
        y_ref[...] = jnp.zeros_like(y_ref)


def _moe_experts(h, src, tile_expert, n_valid, norm_moe_g, wgu, wd):
    max_tiles = src.shape[0]
    src3 = src.reshape(max_tiles, 1, MOE_TILE)
    grid_spec = pltpu.PrefetchScalarGridSpec(
        num_scalar_prefetch=2,
        grid=(max_tiles,),
        in_specs=[
            pl.BlockSpec((1, 1, MOE_TILE), lambda k, te, nv: (k, 0, 0), memory_space=pltpu.SMEM),
            pl.BlockSpec((1, 1, MOE_TILE),
                         lambda k, te, nv: (jnp.minimum(k + 1, max_tiles - 1), 0, 0),
                         memory_space=pltpu.SMEM),
            pl.BlockSpec(memory_space=pl.ANY),
            pl.BlockSpec((1, D_MODEL), lambda k, te, nv: (0, 0)),
            pl.BlockSpec((1, D_MODEL, 2 * D_EXPERT), lambda k, te, nv: (te[k], 0, 0)),
            pl.BlockSpec((1, D_EXPERT, D_MODEL), lambda k, te, nv: (te[k], 0, 0)),
        ],
        out_specs=pl.BlockSpec((MOE_TILE, D_MODEL), lambda k, te, nv: (k, 0)),
        scratch_shapes=[pltpu.VMEM((2, MOE_TILE, D_MODEL), F32), pltpu.SemaphoreType.DMA((2,))],
    )
    return pl.pallas_call(
        _moe_experts_kernel,
        grid_spec=grid_spec,
        out_shape=jax.ShapeDtypeStruct((max_tiles * MOE_TILE, D_MODEL), F32),
        compiler_params=pltpu.CompilerParams(dimension_semantics=("arbitrary",),
                                             vmem_limit_bytes=VMEM_LIMIT),
        name="moe_experts",
    )(tile_expert, n_valid, src3, src3, h, norm_moe_g, wgu, wd)


def _moe_combine_kernel(pos_ref, pos_next_ref, y_hbm, h_ref, route_ref, fg_ref, out_ref,
                        ybuf, sem):
    i = pl.program_id(0)
    rows = h_ref.shape[0]

    @pl.when(i == 0)
    def _():
        _issue_row_gather(pos_ref, TOP_K * rows, y_hbm, ybuf, 0, sem)

    @pl.when(i + 1 < pl.num_programs(0))
    def _():
        _issue_row_gather(pos_next_ref, TOP_K * rows, y_hbm, ybuf, (i + 1) % 2, sem)

    slot = i % 2
    _wait_row_gather(TOP_K * rows, y_hbm, ybuf, slot, sem)
    route = route_ref[...]
    moe = route[:, 2:3] * ybuf[slot, :rows] + route[:, 3:4] * ybuf[slot, rows:]
    out_ref[...] = _rms(h_ref[...] + moe) * fg_ref[...]


def _moe_combine(y, pos, h, route, final_g):
    tokens = h.shape[0]
    rows = COMBINE_ROWS
    n_tiles = tokens // rows
    pos3 = pos.reshape(n_tiles, rows, TOP_K).transpose(0, 2, 1).reshape(n_tiles, 1, TOP_K * rows)
    tok = lambda width: pl.BlockSpec((rows, width), lambda i: (i, 0))
    return pl.pallas_call(
        _moe_combine_kernel,
        grid=(n_tiles,),
        in_specs=[
            pl.BlockSpec((1, 1, TOP_K * rows), lambda i: (i, 0, 0), memory_space=pltpu.SMEM),
            pl.BlockSpec((1, 1, TOP_K * rows), lambda i: (jnp.minimum(i + 1, n_tiles - 1), 0, 0),
                         memory_space=pltpu.SMEM),
            pl.BlockSpec(memory_space=pl.ANY),
            tok(D_MODEL), tok(ROUTE_LANES), pl.BlockSpec((1, D_MODEL), lambda i: (0, 0)),
        ],
        out_specs=tok(D_MODEL),
        out_shape=jax.ShapeDtypeStruct((tokens, D_MODEL), F32),
        scratch_shapes=[pltpu.VMEM((2, TOP_K * rows, D_MODEL), F32),
                        pltpu.SemaphoreType.DMA((2,))],
        compiler_params=pltpu.CompilerParams(dimension_semantics=("arbitrary",),
                                             vmem_limit_bytes=VMEM_LIMIT),
        name="moe_combine",
    )(pos3, pos3, y, h, route, final_g)


def _prepare_weights(norm_mix_g, w_in, v_norm_g, spatial_w, spatial_b, fourier_w, gmlp_out_g,
                     fourier_out_g, w_out, norm_moe_g, router_group_w, router_group_b,
                     router_expert_w, router_expert_b, expert_w_gate, expert_w_up,
                     expert_w_down, final_norm_g):
    w_in, spatial_w, spatial_b, fourier_w = w_in[0], spatial_w[0], spatial_b[0], fourier_w[0]
    wp, wq = _fold_fourier_weights(w_in[:, 2 * D_GMLP:], fourier_w)
    w_ext = jnp.concatenate([w_in[:, :2 * D_GMLP].astype(BF16), wp, wq], axis=1)
    wsp = spatial_w.reshape(N_HEADS // 2, 2, CHUNK, CHUNK).transpose(0, 2, 1, 3)
    wsp = wsp.reshape(N_HEADS // 2, CHUNK, 2 * CHUNK).astype(BF16)
    sbias = jnp.repeat(spatial_b.T, HEAD_DIM, axis=1)
    pad = ROUTE_LANES - N_GROUPS - N_EXPERTS
    wr = jnp.concatenate([router_group_w[0],
                          router_expert_w[0].transpose(1, 0, 2).reshape(D_MODEL, N_EXPERTS),
                          jnp.zeros((D_MODEL, pad), F32)], axis=1)
    br = jnp.concatenate([router_group_b[0], router_expert_b[0].reshape(N_EXPERTS),
                          jnp.zeros((pad,), F32)])[None, :]
    wgu = jnp.concatenate([expert_w_gate[0], expert_w_up[0]], axis=-1).astype(BF16)
    wd = expert_w_down[0].astype(BF16)
    row = lambda a: a.reshape(1, -1)
    return dict(norm_mix_g=row(norm_mix_g), w_ext=w_ext, v_norm_g=row(v_norm_g), wsp=wsp,
                sbias=sbias, gmlp_out_g=row(gmlp_out_g), fourier_out_g=row(fourier_out_g),
                w_out=w_out[0].astype(BF16), norm_moe_g=row(norm_moe_g), wr=wr, br=br,
                wgu=wgu, wd=wd, final_norm_g=row(final_norm_g))


def _trunk(x, w):
    b, seq, _ = x.shape
    tokens = b * seq
    n1 = seq // FFT_N2
    yg, p, q = _mix_in(x.reshape(tokens, D_MODEL), w["norm_mix_g"], w["w_ext"], w["v_norm_g"],
                       w["wsp"], w["sbias"], w["gmlp_out_g"])
    ar, ai = _fft1(p.reshape(b, n1, FFT_N2 * D_FOURIER), q.reshape(b, n1, FFT_N2 * D_FOURIER))
    h, route = _mix_out(ar.reshape(b, seq, D_FOURIER), ai.reshape(b, seq, D_FOURIER),
                        yg.reshape(b, seq, D_GMLP), x, w["fourier_out_g"], w["w_out"],
                        w["norm_moe_g"], w["wr"], w["br"])
    pos, src, tile_expert, n_valid = _route_plan(route)
    y = _moe_experts(h, src, tile_expert, n_valid, w["norm_moe_g"], w["wgu"], w["wd"])
    out = _moe_combine(y, pos, h, route, w["final_norm_g"])
    return out.reshape(b, seq, D_MODEL)


def kernel(x_prompt, x_sample, norm_mix_g, w_in, v_norm_g, spatial_w, spatial_b, fourier_w,
           gmlp_out_g, fourier_out_g, w_out, norm_moe_g, router_group_w, router_group_b,
           router_expert_w, router_expert_b, expert_w_gate, expert_w_up, expert_w_down,
           final_norm_g):
    w = _prepare_weights(norm_mix_g, w_in, v_norm_g, spatial_w, spatial_b, fourier_w, gmlp_out_g,
                         fourier_out_g, w_out, norm_moe_g, router_group_w, router_group_b,
                         router_expert_w, router_expert_b, expert_w_gate, expert_w_up,
                         expert_w_down, final_norm_g)
    return (_trunk(x_prompt, w), _trunk(x_sample, w))
```
